```python
import jax, jax.numpy as jnp
from jax import lax
import numpy as np

D_MODEL = 1024
BATCH = 16
SEQ = 2048
DEPTH = 1

D_MIX = D_MODEL
HEAD_DIM = 64
ATTN_WIDTH = D_MIX // 2
N_ATTN_HEADS = ATTN_WIDTH // HEAD_DIM
CONV_WIDTH = D_MIX - ATTN_WIDTH
CONV_GROUP_DIM = 64
N_CONV_GROUPS = CONV_WIDTH // CONV_GROUP_DIM
CONV_K = 3
DILATED_PATTERNS = ((128, 1), (512, 4), (2048, 16))
ATTN_BLOCK = 128
IN_PROJ_WIDTH = 3 * ATTN_WIDTH + 3 * CONV_WIDTH
N_EXPERTS = 64
TOP_K = 8
N_GROUPS = 8
TOPK_GROUPS = 4
D_EXPERT = 256
D_SHARED = 256
ROUTED_SCALE = 2.5
MOE_TOKEN_BLOCK = 128
EPS = 1e-6
NEG_INF = -1e30

kernel_name = "hybrid_dilated_attn_shortconv_moe_adaln"


def rms_norm(x):
    xf = x.astype(jnp.float32)
    return (xf * lax.rsqrt(jnp.mean(xf * xf, axis=-1, keepdims=True) + EPS)).astype(x.dtype)


def modulate(h, shift, scale):
    return h * (1.0 + scale[:, None, :]) + shift[:, None, :]


def dilated_window_attention(q, k, v, window, dilation):
    b, t, h, dh = q.shape
    n_sub = window // dilation
    L = t // dilation
    nb = -(-L // ATTN_BLOCK)
    lp = nb * ATTN_BLOCK
    pad_end = lp - L

    def to_residue(a):
        return a.reshape(b, L, dilation, h, dh).transpose(0, 2, 1, 3, 4)

    qs, ks, vs = to_residue(q), to_residue(k), to_residue(v)
    qs = jnp.pad(qs, ((0, 0), (0, 0), (0, pad_end), (0, 0), (0, 0)))
    kv_pad = ((0, 0), (0, 0), (ATTN_BLOCK, pad_end), (0, 0), (0, 0))
    ks, vs = jnp.pad(ks, kv_pad), jnp.pad(vs, kv_pad)
    qb = qs.reshape(b, dilation, nb, ATTN_BLOCK, h, dh)

    def band(a):
        a = a.reshape(b, dilation, nb + 1, ATTN_BLOCK, h, dh)
        return jnp.concatenate([a[:, :, :-1], a[:, :, 1:]], axis=3)

    kb, vb = band(ks), band(vs)
    scale = 1.0 / np.sqrt(HEAD_DIM).astype(np.float32)
    s = jnp.einsum('bgnqhd,bgnkhd->bgnhqk', qb, kb) * scale
    qi = jnp.arange(ATTN_BLOCK)[:, None]
    ki = jnp.arange(2 * ATTN_BLOCK)[None, :]
    dist = qi + ATTN_BLOCK - ki
    key_pos = jnp.arange(nb)[:, None, None] * ATTN_BLOCK + ki[None] - ATTN_BLOCK
    valid = (dist >= 0)[None] & (dist <= n_sub)[None] & (key_pos >= 0)
    s = jnp.where(valid[:, None], s, NEG_INF)
    lse = jax.nn.logsumexp(s, axis=-1)
    p = jnp.exp(s - lse[..., None])
    o = jnp.einsum('bgnhqk,bgnkhd->bgnqhd', p, vb)
    o = o.reshape(b, dilation, lp, h, dh)[:, :, :L]
    o = o.transpose(0, 2, 1, 3, 4).reshape(b, t, h, dh)
    lse = lse.transpose(0, 1, 2, 4, 3).reshape(b, dilation, lp, h)[:, :, :L]
    lse = lse.transpose(0, 2, 1, 3).reshape(b, t, h)
    return o, lse


def mixture_of_dilations(q, k, v):
    outs, lses = [], []
    for window, dilation in DILATED_PATTERNS:
        o, l = dilated_window_attention(q, k, v, window, dilation)
        outs.append(o)
        lses.append(l)
    wts = jax.nn.softmax(jnp.stack(lses, axis=0), axis=0)
    return jnp.einsum('pbth,pbthd->bthd', wts, jnp.stack(outs, axis=0))


def causal_depthwise_conv(z, w):
    t = z.shape[1]
    zp = jnp.pad(z, ((0, 0), (CONV_K - 1, 0), (0, 0)))
    return sum(w[j] * zp[:, j:j + t] for j in range(CONV_K))


def group_rms(y, gain, group_dim):
    b, t, d = y.shape
    yg = rms_norm(y.reshape(b, t, d // group_dim, group_dim)).reshape(b, t, d)
    return yg * gain


def token_mixer(h, w_in, q_norm_g, k_norm_g, conv_w, attn_out_g, conv_out_g, w_out):
    b, t, _ = h.shape
    proj = h @ w_in
    a0, a1, a2 = ATTN_WIDTH, 2 * ATTN_WIDTH, 3 * ATTN_WIDTH
    q = proj[..., :a0].reshape(b, t, N_ATTN_HEADS, HEAD_DIM)
    k = proj[..., a0:a1].reshape(b, t, N_ATTN_HEADS, HEAD_DIM)
    v = proj[..., a1:a2].reshape(b, t, N_ATTN_HEADS, HEAD_DIM)
    gate_b = proj[..., a2:a2 + CONV_WIDTH]
    gate_c = proj[..., a2 + CONV_WIDTH:a2 + 2 * CONV_WIDTH]
    u = proj[..., a2 + 2 * CONV_WIDTH:]

    qf = (rms_norm(q) * q_norm_g).astype(jnp.float32)
    kf = (rms_norm(k) * k_norm_g).astype(jnp.float32)
    attn = mixture_of_dilations(qf, kf, v.astype(jnp.float32)).astype(h.dtype)
    attn = attn.reshape(b, t, ATTN_WIDTH)

    conv = gate_b * causal_depthwise_conv(gate_c * u, conv_w)

    y = jnp.concatenate([group_rms(attn, attn_out_g, HEAD_DIM),
                         group_rms(conv, conv_out_g, CONV_GROUP_DIM)], axis=-1)
    return y @ w_out


def moe_ffn(h, w_router, router_bias, w_e_gate, w_e_up, w_e_down, w_s_gate, w_s_up, w_s_down):
    b, t, d = h.shape
    n = b * t
    xt = h.reshape(n, d)
    scores = jax.nn.sigmoid(xt.astype(jnp.float32) @ w_router.astype(jnp.float32))
    biased = scores + router_bias.astype(jnp.float32)
    grp = biased.reshape(n, N_GROUPS, N_EXPERTS // N_GROUPS)
    grp_score = lax.top_k(grp, 2)[0].sum(-1)
    _, gidx = lax.top_k(grp_score, TOPK_GROUPS)
    gmask = jnp.max(jax.nn.one_hot(gidx, N_GROUPS, dtype=jnp.float32), axis=1) > 0
    emask = jnp.repeat(gmask, N_EXPERTS // N_GROUPS, axis=1)
    _, eidx = lax.top_k(jnp.where(emask, biased, NEG_INF), TOP_K)
    w_sel = jnp.take_along_axis(scores, eidx, axis=1)
    w_sel = w_sel / jnp.sum(w_sel, axis=-1, keepdims=True) * ROUTED_SCALE
    gates = jnp.sum(jax.nn.one_hot(eidx, N_EXPERTS, dtype=jnp.float32) * w_sel[..., None], axis=1)
    gates = gates.astype(h.dtype)

    def expert_block(args):
        xb, gb = args
        hg = jnp.einsum('nd,edf->nef', xb, w_e_gate)
        hu = jnp.einsum('nd,edf->nef', xb, w_e_up)
        hid = jax.nn.silu(hg) * hu * gb[..., None]
        return jnp.einsum('nef,efd->nd', hid, w_e_down)

    nblk = n // MOE_TOKEN_BLOCK
    routed = lax.map(expert_block, (xt.reshape(nblk, MOE_TOKEN_BLOCK, d),
                                    gates.reshape(nblk, MOE_TOKEN_BLOCK, N_EXPERTS)))
    shared = (jax.nn.silu(xt @ w_s_gate) * (xt @ w_s_up)) @ w_s_down
    return (routed.reshape(n, d) + shared).reshape(b, t, d)


def hybrid_layer(x, c, w_ada, b_ada, w_in, q_norm_g, k_norm_g, conv_w, attn_out_g, conv_out_g,
                 w_out, w_router, router_bias, w_e_gate, w_e_up, w_e_down, w_s_gate, w_s_up, w_s_down):
    mod = jax.nn.silu(c) @ w_ada + b_ada
    sh1, sc1, g1, sh2, sc2, g2 = jnp.split(mod, 6, axis=-1)
    h = modulate(rms_norm(x), sh1, sc1)
    x = x + g1[:, None, :] * token_mixer(h, w_in, q_norm_g, k_norm_g, conv_w,
                                         attn_out_g, conv_out_g, w_out)
    h = modulate(rms_norm(x), sh2, sc2)
    x = x + g2[:, None, :] * moe_ffn(h, w_router, router_bias, w_e_gate, w_e_up, w_e_down,
                                     w_s_gate, w_s_up, w_s_down)
    return x


def setup_inputs(seed: int = 0) -> dict:
    key = jax.random.key(seed)
    ks = jax.random.split(key, 20)
    f32 = jnp.float32
    L = DEPTH
    nrm = lambda k, shape, s: (jax.random.normal(k, shape, f32) * s).astype(f32)
    return {
        "x": nrm(ks[0], (BATCH, SEQ, D_MODEL), 1.0),
        "c": nrm(ks[1], (BATCH, D_MODEL), 1.0),
        "w_ada": nrm(ks[2], (L, D_MODEL, 6 * D_MODEL), 0.5 * D_MODEL ** -0.5),
        "b_ada": nrm(ks[3], (L, 6 * D_MODEL), 0.1),
        "w_in": nrm(ks[4], (L, D_MODEL, IN_PROJ_WIDTH), D_MODEL ** -0.5),
        "q_norm_g": 1.0 + nrm(ks[5], (L, HEAD_DIM), 0.02),
        "k_norm_g": 1.0 + nrm(ks[6], (L, HEAD_DIM), 0.02),
        "conv_w": nrm(ks[7], (L, CONV_K, CONV_WIDTH), CONV_K ** -0.5),
        "attn_out_g": 1.0 + nrm(ks[8], (L, ATTN_WIDTH), 0.02),
        "conv_out_g": 1.0 + nrm(ks[9], (L, CONV_WIDTH), 0.02),
        "w_out": nrm(ks[10], (L, D_MIX, D_MODEL), D_MIX ** -0.5),
        "w_router": nrm(ks[11], (L, D_MODEL, N_EXPERTS), D_MODEL ** -0.5),
        "router_bias": nrm(ks[12], (L, N_EXPERTS), 0.01),
        "w_e_gate": nrm(ks[13], (L, N_EXPERTS, D_MODEL, D_EXPERT), D_MODEL ** -0.5),
        "w_e_up": nrm(ks[14], (L, N_EXPERTS, D_MODEL, D_EXPERT), D_MODEL ** -0.5),
        "w_e_down": nrm(ks[15], (L, N_EXPERTS, D_EXPERT, D_MODEL), D_EXPERT ** -0.5),
        "w_s_gate": nrm(ks[16], (L, D_MODEL, D_SHARED), D_MODEL ** -0.5),
        "w_s_up": nrm(ks[17], (L, D_MODEL, D_SHARED), D_MODEL ** -0.5),
        "w_s_down": nrm(ks[18], (L, D_SHARED, D_MODEL), D_SHARED ** -0.5),
    }


def reference(x, c, w_ada, b_ada, w_in, q_norm_g, k_norm_g, conv_w, attn_out_g, conv_out_g,
              w_out, w_router, router_bias, w_e_gate, w_e_up, w_e_down, w_s_gate, w_s_up, w_s_down):
    for l in range(DEPTH):
        x = hybrid_layer(x, c, w_ada[l], b_ada[l], w_in[l], q_norm_g[l], k_norm_g[l], conv_w[l],
                         attn_out_g[l], conv_out_g[l], w_out[l], w_router[l], router_bias[l],
                         w_e_gate[l], w_e_up[l], w_e_down[l], w_s_gate[l], w_s_up[l], w_s_down[l])
    return x
```

```python
import functools

import jax
import jax.numpy as jnp
from jax import lax
from jax.experimental import pallas as pl
from jax.experimental.pallas import tpu as pltpu

F32 = jnp.float32
BF16 = jnp.bfloat16

HEAD_DIM = 64
N_HEADS = 8
ATTN_WIDTH = HEAD_DIM * N_HEADS
CONV_WIDTH = 512
CONV_K = 3
PATTERNS = ((128, 1), (512, 4), (2048, 16))
ATTN_BLOCK = 128
N_EXPERTS = 64
TOP_K = 8
N_GROUPS = 8
GROUP_SIZE = N_EXPERTS // N_GROUPS
TOPK_GROUPS = 4
D_EXPERT = 256
ROUTED_SCALE = 2.5
EPS = 1e-6
NEG_INF = -1e30

LANES = 128
VMEM_LIMIT = 52 * 1024 * 1024


def _cparams(sem):
    return pltpu.CompilerParams(dimension_semantics=sem, vmem_limit_bytes=VMEM_LIMIT)


def _adaln_kernel(c_ref, w_ref, b_ref, o_ref):
    c = c_ref[...]
    a = c * jax.nn.sigmoid(c)
    o_ref[...] = jnp.dot(a, w_ref[...], precision=lax.Precision.HIGHEST,
                         preferred_element_type=F32) + b_ref[...]


def _adaln(c, w, b):
    bsz, d = c.shape
    n = w.shape[1]
    tn = 1536
    return pl.pallas_call(
        _adaln_kernel,
        out_shape=jax.ShapeDtypeStruct((bsz, n), F32),
        grid=(n // tn,),
        in_specs=[pl.BlockSpec((bsz, d), lambda j: (0, 0)),
                  pl.BlockSpec((d, tn), lambda j: (0, j)),
                  pl.BlockSpec((1, tn), lambda j: (0, j))],
        out_specs=pl.BlockSpec((bsz, tn), lambda j: (0, j)),
        compiler_params=_cparams(("arbitrary",)),
        name="adaln",
    )(c, w, b.reshape(1, n))


def _group_mean_sq(v, bd):
    return jnp.dot((v * v).astype(BF16), bd, preferred_element_type=F32)


def _inproj_kernel(x_ref, mod_ref, w_ref, qg_ref, kg_ref, cw_ref, cg_ref, bd_ref,
                   q_ref, k_ref, v_ref, yc_ref, ztail_ref):
    i = pl.program_id(1)
    tm = x_ref.shape[1]
    x = x_ref[0]
    xn = x * lax.rsqrt(jnp.mean(x * x, axis=-1, keepdims=True) + EPS)
    h = (xn * (1.0 + mod_ref[0, 1:2, :]) + mod_ref[0, 0:1, :]).astype(BF16)
    bd = bd_ref[...]

    def proj(j):
        return jnp.dot(h, w_ref[:, j * 512:(j + 1) * 512], preferred_element_type=F32)

    q = proj(0)
    q_ref[0] = (q * lax.rsqrt(_group_mean_sq(q, bd) + EPS) * qg_ref[...]).astype(BF16)
    k = proj(1)
    k_ref[0] = (k * lax.rsqrt(_group_mean_sq(k, bd) + EPS) * kg_ref[...]).astype(BF16)
    v_ref[0] = proj(2).astype(BF16)

    z = proj(4) * proj(5)
    row = lax.broadcasted_iota(jnp.int32, z.shape, 0)
    @pl.when(i == 0)
    def _():
        ztail_ref[...] = jnp.zeros_like(ztail_ref)

    p1 = ztail_ref[7:8, :]
    p2 = ztail_ref[6:7, :]
    z1 = jnp.where(row == 0, p1, pltpu.roll(z, 1, axis=0))
    z2 = jnp.where(row == 0, p2, jnp.where(row == 1, p1, pltpu.roll(z, 2, axis=0)))
    ztail_ref[...] = z[tm - 8:tm, :]
    y = cw_ref[0:1, :] * z2 + cw_ref[1:2, :] * z1 + cw_ref[2:3, :] * z
    conv = proj(3) * y
    yc_ref[0] = (conv * lax.rsqrt(_group_mean_sq(conv, bd) + EPS) * cg_ref[...]).astype(BF16)


def _inproj(x, mod, w_in, qg, kg, conv_w, conv_g, bd, tm=512):
    bsz, t, d = x.shape
    wtot = w_in.shape[1]
    row = lambda b, i: (b, i, 0)
    const = lambda b, i: (0, 0)
    out = jax.ShapeDtypeStruct((bsz, t, 512), BF16)
    return pl.pallas_call(
        _inproj_kernel,
        out_shape=(out, out, out, out),
        grid=(bsz, t // tm),
        in_specs=[pl.BlockSpec((1, tm, d), row),
                  pl.BlockSpec((1, 6, d), lambda b, i: (b, 0, 0)),
                  pl.BlockSpec((d, wtot), const),
                  pl.BlockSpec((1, 512), const),
                  pl.BlockSpec((1, 512), const),
                  pl.BlockSpec((CONV_K, 512), const),
                  pl.BlockSpec((1, 512), const),
                  pl.BlockSpec((512, 512), const)],
        out_specs=(pl.BlockSpec((1, tm, 512), row),) * 4,
        scratch_shapes=[pltpu.VMEM((8, 512), F32)],
        compiler_params=_cparams(("arbitrary", "arbitrary")),
        name="inproj",
    )(x, mod, w_in, qg, kg, conv_w, conv_g, bd)


def _attn_unit(q, kb, vb, valid, lane_lo):
    zero = jnp.zeros_like(q)
    qq = jnp.concatenate([jnp.where(lane_lo, q, zero), jnp.where(lane_lo, zero, q)], axis=0)
    s = lax.dot_general(qq, kb, (((1,), (1,)), ((), ())), preferred_element_type=F32)
    s = jnp.where(valid, s, NEG_INF)
    m = jnp.max(s, axis=1, keepdims=True)
    p = jnp.exp(s - m)
    l = jnp.sum(p, axis=1, keepdims=True)
    o = jnp.dot(p.astype(BF16), vb, preferred_element_type=F32) * (1.0 / l)
    lse = m + jnp.log(l)
    o = jnp.where(lane_lo, o[:ATTN_BLOCK], o[ATTN_BLOCK:])
    return o, lse[:ATTN_BLOCK], lse[ATTN_BLOCK:]


def _attn_kernel(q_ref, k_ref, v_ref, o_ref, lse_ref, *, nb):
    blk = ATTN_BLOCK
    ncol = q_ref.shape[2] // LANES
    lane_lo = lax.broadcasted_iota(jnp.int32, (blk, LANES), 1) < HEAD_DIM
    lane8 = lax.broadcasted_iota(jnp.int32, (blk, 2 * ncol), 1)
    qi = lax.broadcasted_iota(jnp.int32, (2 * blk, blk), 0) & (blk - 1)
    ki = lax.broadcasted_iota(jnp.int32, (2 * blk, blk), 1)
    valid_own = ki <= qi
    qi2 = lax.broadcasted_iota(jnp.int32, (2 * blk, 2 * blk), 0) & (blk - 1)
    ci2 = lax.broadcasted_iota(jnp.int32, (2 * blk, 2 * blk), 1)
    valid_band = (ci2 >= qi2) & (ci2 <= qi2 + blk)

    def do_block(r0, k0, nk, valid):
        lse8 = jnp.zeros((blk, 2 * ncol), F32)
        for c in range(ncol):
            cs = slice(c * LANES, (c + 1) * LANES)
            o, l0, l1 = _attn_unit(q_ref[0, pl.ds(r0, blk), cs], k_ref[0, pl.ds(k0, nk), cs],
                                   v_ref[0, pl.ds(k0, nk), cs], valid, lane_lo)
            o_ref[0, pl.ds(r0, blk), cs] = o.astype(BF16)
            lse8 = jnp.where(lane8 == 2 * c, l0, jnp.where(lane8 == 2 * c + 1, l1, lse8))
        lse_ref[0, 0, pl.ds(r0, blk), :] = lse8

    do_block(0, 0, blk, valid_own)

    if nb > 1:
        def body(j, carry):
            r0 = pl.multiple_of(j * blk, blk)
            do_block(r0, pl.multiple_of(r0 - blk, blk), 2 * blk, valid_band)
            return carry
        lax.fori_loop(1, nb, body, 0)


def _attention_pattern(q, k, v, dilation):
    bsz, t, w = q.shape
    sub_len = t // dilation
    nb = sub_len // ATTN_BLOCK
    view = lambda a: a.reshape(bsz, sub_len, dilation * w)
    spec = pl.BlockSpec((1, sub_len, w), lambda b, r: (b, 0, r))
    o, lse = pl.pallas_call(
        functools.partial(_attn_kernel, nb=nb),
        out_shape=(jax.ShapeDtypeStruct((bsz, sub_len, dilation * w), BF16),
                   jax.ShapeDtypeStruct((bsz, dilation, sub_len, N_HEADS), F32)),
        grid=(bsz, dilation),
        in_specs=[spec, spec, spec],
        out_specs=(spec, pl.BlockSpec((1, 1, sub_len, N_HEADS), lambda b, r: (b, r, 0, 0))),
        compiler_params=_cparams(("arbitrary", "arbitrary")),
        name=f"attn_d{dilation}",
    )(view(q), view(k), view(v))
    return o.reshape(bsz, t, w), lse.transpose(0, 2, 1, 3).reshape(bsz, t, N_HEADS)


def _outproj_kernel(o1_ref, o2_ref, o3_ref, l1_ref, l2_ref, l3_ref, yc_ref, x_ref, mod_ref,
                    wo_ref, ag_ref, bd_ref, wr_ref, x1_ref, h2_ref, sc_ref):
    tm = x_ref.shape[1]
    l1, l2, l3 = l1_ref[0], l2_ref[0], l3_ref[0]
    m = jnp.maximum(jnp.maximum(l1, l2), l3)
    e1, e2, e3 = jnp.exp(l1 - m), jnp.exp(l2 - m), jnp.exp(l3 - m)
    inv = 1.0 / (e1 + e2 + e3)
    wts = (e1 * inv, e2 * inv, e3 * inv)
    lane_lo = lax.broadcasted_iota(jnp.int32, (tm, LANES), 1) < HEAD_DIM
    cols = []
    for c in range(ATTN_WIDTH // LANES):
        cs = slice(c * LANES, (c + 1) * LANES)
        acc = jnp.zeros((tm, LANES), F32)
        for w, o_ref in zip(wts, (o1_ref, o2_ref, o3_ref)):
            wexp = jnp.where(lane_lo, w[:, 2 * c:2 * c + 1], w[:, 2 * c + 1:2 * c + 2])
            acc = acc + wexp * o_ref[0, :, cs].astype(F32)
        cols.append(acc)
    attn = jnp.concatenate(cols, axis=1)
    attn = attn * lax.rsqrt(_group_mean_sq(attn, bd_ref[...]) + EPS) * ag_ref[...]
    mix = jnp.dot(attn.astype(BF16), wo_ref[0:ATTN_WIDTH, :], preferred_element_type=F32)
    mix = mix + jnp.dot(yc_ref[0], wo_ref[ATTN_WIDTH:, :], preferred_element_type=F32)
    x1 = x_ref[0] + mod_ref[0, 2:3, :] * mix
    x1_ref[0] = x1
    xn = x1 * lax.rsqrt(jnp.mean(x1 * x1, axis=-1, keepdims=True) + EPS)
    h2 = (xn * (1.0 + mod_ref[0, 4:5, :]) + mod_ref[0, 3:4, :]).astype(BF16)
    h2_ref[0] = h2
    logits_t = lax.dot_general(wr_ref[...], h2, (((1,), (1,)), ((), ())),
                               preferred_element_type=F32)
    sc_ref[0] = jax.nn.sigmoid(logits_t)


def _outproj(o1, o2, o3, l1, l2, l3, yc, x, mod, w_out, ag, bd, wr_t, tm=512):
    bsz, t, d = x.shape
    row = lambda b, i: (b, i, 0)
    const = lambda b, i: (0, 0)
    ospec = pl.BlockSpec((1, tm, 512), row)
    lspec = pl.BlockSpec((1, tm, N_HEADS), row)
    xspec = pl.BlockSpec((1, tm, d), row)
    return pl.pallas_call(
        _outproj_kernel,
        out_shape=(jax.ShapeDtypeStruct((bsz, t, d), F32),
                   jax.ShapeDtypeStruct((bsz, t, d), BF16),
                   jax.ShapeDtypeStruct((bsz, N_EXPERTS, t), F32)),
        grid=(bsz, t // tm),
        in_specs=[ospec, ospec, ospec, lspec, lspec, lspec, ospec, xspec,
                  pl.BlockSpec((1, 6, d), lambda b, i: (b, 0, 0)),
                  pl.BlockSpec((d, d), const),
                  pl.BlockSpec((1, 512), const),
                  pl.BlockSpec((512, 512), const),
                  pl.BlockSpec((N_EXPERTS, d), const)],
        out_specs=(xspec, xspec, pl.BlockSpec((1, N_EXPERTS, tm), lambda b, i: (b, 0, i))),
        compiler_params=_cparams(("arbitrary", "arbitrary")),
        name="outproj",
    )(o1, o2, o3, l1, l2, l3, yc, x, mod, w_out, ag, bd, wr_t)


def _first_index(match, ids, sentinel):
    return jnp.min(jnp.where(match, ids, sentinel), axis=0, keepdims=True)


def _route_kernel(s_ref, bias_ref, g_ref):
    tt = s_ref.shape[2]
    gsz = GROUP_SIZE
    sub = lax.broadcasted_iota(jnp.int32, (gsz, tt), 0).astype(F32)
    scores = [s_ref[0, g * gsz:(g + 1) * gsz, :] for g in range(N_GROUPS)]
    biased = [scores[g] + bias_ref[g * gsz:(g + 1) * gsz, :] for g in range(N_GROUPS)]

    rows = []
    for g in range(N_GROUPS):
        b = biased[g]
        m1 = jnp.max(b, axis=0, keepdims=True)
        i1 = _first_index(b == m1, sub, float(gsz))
        m2 = jnp.max(jnp.where(sub == i1, -jnp.inf, b), axis=0, keepdims=True)
        rows.append(m1 + m2)
    gscore = jnp.concatenate(rows, axis=0)

    gmask = jnp.zeros((N_GROUPS, tt), F32)
    for _ in range(TOPK_GROUPS):
        gm = jnp.max(gscore, axis=0, keepdims=True)
        gi = _first_index(gscore == gm, sub, float(N_GROUPS))
        hit = sub == gi
        gmask = jnp.where(hit, 1.0, gmask)
        gscore = jnp.where(hit, -jnp.inf, gscore)

    cur = [jnp.where(gmask[g:g + 1, :] > 0.5, biased[g], NEG_INF) for g in range(N_GROUPS)]
    eid = [sub + float(g * gsz) for g in range(N_GROUPS)]
    sel = [jnp.zeros((gsz, tt), F32) for _ in range(N_GROUPS)]
    for _ in range(TOP_K):
        em = jnp.max(cur[0], axis=0, keepdims=True)
        for g in range(1, N_GROUPS):
            em = jnp.maximum(em, jnp.max(cur[g], axis=0, keepdims=True))
        ei = _first_index(cur[0] == em, eid[0], float(N_EXPERTS))
        for g in range(1, N_GROUPS):
            ei = jnp.minimum(ei, _first_index(cur[g] == em, eid[g], float(N_EXPERTS)))
        for g in range(N_GROUPS):
            hit = eid[g] == ei
            sel[g] = jnp.where(hit, 1.0, sel[g])
            cur[g] = jnp.where(hit, -jnp.inf, cur[g])

    w = [sel[g] * scores[g] for g in range(N_GROUPS)]
    wsum = jnp.sum(w[0], axis=0, keepdims=True)
    for g in range(1, N_GROUPS):
        wsum = wsum + jnp.sum(w[g], axis=0, keepdims=True)
    for g in range(N_GROUPS):
        g_ref[0, g * gsz:(g + 1) * gsz, :] = w[g] / wsum * ROUTED_SCALE


def _route(scores_t, bias, tt=1024):
    bsz, ne, t = scores_t.shape
    spec = pl.BlockSpec((1, ne, tt), lambda b, i: (b, 0, i))
    return pl.pallas_call(
        _route_kernel,
        out_shape=jax.ShapeDtypeStruct((bsz, ne, t), F32),
        grid=(bsz, t // tt),
        in_specs=[spec, pl.BlockSpec((ne, 1), lambda b, i: (0, 0))],
        out_specs=spec,
        compiler_params=_cparams(("arbitrary", "arbitrary")),
        name="route",
    )(scores_t, bias.reshape(ne, 1))


def _moe_kernel(h_ref, g_ref, wgu_ref, wd_ref, x1_ref, mod_ref, o_ref, acc_ref):
    e = pl.program_id(1)
    tm = h_ref.shape[0]
    lane = lax.broadcasted_iota(jnp.int32, (tm, LANES), 1)
    gate = jnp.sum(jnp.where(lane == e, g_ref[...], 0.0), axis=1, keepdims=True)
    hgu = jnp.dot(h_ref[...], wgu_ref[0], preferred_element_type=F32)
    hg, hu = hgu[:, :D_EXPERT], hgu[:, D_EXPERT:]
    hid = (hg * jax.nn.sigmoid(hg) * hu * gate).astype(BF16)
    y = jnp.dot(hid, wd_ref[0], preferred_element_type=F32)

    @pl.when(e == 0)
    def _():
        acc_ref[...] = y

    @pl.when(e > 0)
    def _():
        acc_ref[...] += y

    @pl.when(e == pl.num_programs(1) - 1)
    def _():
        o_ref[...] = x1_ref[...] + mod_ref[0, 5:6, :] * acc_ref[...]


def _moe_dense(h2, gates, wgu, wd, x1, mod, t, tm=1024):
    n, d = h2.shape
    ne = wgu.shape[0]
    per_b = t // tm
    row = lambda i, e: (i, 0)
    return pl.pallas_call(
        _moe_kernel,
        out_shape=jax.ShapeDtypeStruct((n, d), F32),
        grid=(n // tm, ne),
        in_specs=[pl.BlockSpec((tm, d), row),
                  pl.BlockSpec((tm, LANES), row),
                  pl.BlockSpec((1, d, 2 * D_EXPERT), lambda i, e: (e, 0, 0)),
                  pl.BlockSpec((1, D_EXPERT, d), lambda i, e: (e, 0, 0)),
                  pl.BlockSpec((tm, d), row),
                  pl.BlockSpec((1, 6, d), lambda i, e: (i // per_b, 0, 0))],
        out_specs=pl.BlockSpec((tm, d), row),
        scratch_shapes=[pltpu.VMEM((tm, d), F32)],
        compiler_params=_cparams(("arbitrary", "arbitrary")),
        name="moe",
    )(h2, gates, wgu, wd, x1, mod)


def _layer(x, c, w_ada, b_ada, w_in, q_norm_g, k_norm_g, conv_w, attn_out_g, conv_out_g,
           w_out, w_router, router_bias, w_e_gate, w_e_up, w_e_down, w_s_gate, w_s_up, w_s_down):
    bsz, t, d = x.shape
    n = bsz * t
    mod = _adaln(c, w_ada, b_ada).reshape(bsz, 6, d)

    group = jnp.arange(512) // HEAD_DIM
    bd = jnp.where(group[:, None] == group[None, :], 1.0 / HEAD_DIM, 0.0).astype(BF16)
    qg = (jnp.tile(q_norm_g, N_HEADS) * (HEAD_DIM ** -0.5)).reshape(1, 512)
    kg = jnp.tile(k_norm_g, N_HEADS).reshape(1, 512)
    q, k, v, yc = _inproj(x, mod, w_in.astype(BF16), qg, kg, conv_w,
                          conv_out_g.reshape(1, 512), bd)

    outs = [_attention_pattern(q, k, v, dil) for _, dil in PATTERNS]
    (o1, l1), (o2, l2), (o3, l3) = outs

    x1, h2, scores_t = _outproj(o1, o2, o3, l1, l2, l3, yc, x, mod, w_out.astype(BF16),
                                attn_out_g.reshape(1, 512), bd, w_router.T.astype(BF16))

    gates_t = _route(scores_t, router_bias)
    gates = gates_t.transpose(0, 2, 1).reshape(n, N_EXPERTS)
    gates = jnp.concatenate([gates, jnp.ones((n, 1), F32),
                             jnp.zeros((n, LANES - N_EXPERTS - 1), F32)], axis=1)
    wgu = jnp.concatenate([jnp.concatenate([w_e_gate, w_e_up], axis=2),
                           jnp.concatenate([w_s_gate, w_s_up], axis=1)[None]], axis=0).astype(BF16)
    wd = jnp.concatenate([w_e_down, w_s_down[None]], axis=0).astype(BF16)
    out = _moe_dense(h2.reshape(n, d), gates, wgu, wd, x1.reshape(n, d), mod, t)
    return out.reshape(bsz, t, d)


def kernel(x, c, w_ada, b_ada, w_in, q_norm_g, k_norm_g, conv_w, attn_out_g, conv_out_g, w_out,
           w_router, router_bias, w_e_gate, w_e_up, w_e_down, w_s_gate, w_s_up, w_s_down):
    for l in range(w_ada.shape[0]):
        x = _layer(x, c, w_ada[l], b_ada[l], w_in[l], q_norm_g[l], k_norm_g[l], conv_w[l],
                   attn_out_g[l], conv_out_g[l], w_out[l], w_router[l], router_bias[l],
                   w_e_gate[l], w_e_up[l], w_e_down[l], w_s_gate[l], w_s_up[l], w_s_down[l])
    return x
```
